```python
import jax
import jax.numpy as jnp
from jax import lax
import numpy as np

D_MODEL = 2048
BATCH = 32
SEQ = 256
DEPTH = 2
DEC_BATCH = 4
DEC_SEQ = 2048
PAST_LEN = 256

GRID_W = 64
D_MIX = D_MODEL
HEAD_A = 64
D_A = D_MIX // 4
H_A = D_A // HEAD_A
R_W = 64
R_A = 64
R_G = 128
DECAY_SCALE = 0.606531
GN_EPS_A = 64e-5
D_B = 3 * D_MIX // 8
LRU_BLOCK = 64
NB_B = D_B // LRU_BLOCK
LRU_C = 8.0
CONV_W = 4
CONV_LEFT = CONV_W // 2
D_C = D_MIX - D_A - D_B
P_C = 64
H_C = D_C // P_C
G_C = 2
N_C = 128
CHUNK = 128
N_EXPERTS = 16
CAP_FACTOR = 2
D_EXPERT = D_MODEL // 2
ALPHA = (2.0 * DEPTH) ** 0.25
BETA = (8.0 * DEPTH) ** -0.25
LN_EPS = 1e-5
RMS_EPS = 1e-5
RWKV_SIZES = [D_A, D_A, D_A, R_W, R_W, R_A, R_A, R_G]
RWKV_FEAT = sum(RWKV_SIZES)
SSD_CONV_CH = D_C + 2 * G_C * N_C
IN_SIZES = [RWKV_FEAT, D_B, D_B, D_C, SSD_CONV_CH, H_C, H_C]
N_IN = sum(IN_SIZES)
RWKV_SPLITS = [int(v) for v in np.cumsum(RWKV_SIZES)[:-1]]
IN_SPLITS = [int(v) for v in np.cumsum(IN_SIZES)[:-1]]
F32 = jnp.float32

kernel_name = 'hybrid_rwkv7_rglru_ssd_ec_diffusion_step'


def layer_norm(x, g, b):
    xf = x.astype(F32)
    mu = jnp.mean(xf, axis=-1, keepdims=True)
    var = jnp.mean(jnp.square(xf - mu), axis=-1, keepdims=True)
    return ((xf - mu) * lax.rsqrt(var + LN_EPS) * g + b).astype(x.dtype)


def grid_pos_embed(n_tokens, dtype):
    rows = n_tokens // GRID_W
    quarter = D_MODEL // 4
    omega = 1.0 / (10000.0 ** (jnp.arange(quarter, dtype=F32) / quarter))
    r_idx = jnp.broadcast_to(jnp.arange(rows, dtype=F32)[:, None], (rows, GRID_W)).reshape(-1)
    c_idx = jnp.broadcast_to(jnp.arange(GRID_W, dtype=F32)[None, :], (rows, GRID_W)).reshape(-1)
    ang_r = r_idx[:, None] * omega[None]
    ang_c = c_idx[:, None] * omega[None]
    pe = jnp.concatenate([jnp.sin(ang_r), jnp.cos(ang_r), jnp.sin(ang_c), jnp.cos(ang_c)], axis=-1)
    return pe.astype(dtype)


def centred_shift_mix(u, mu):
    up = jnp.pad(u, ((0, 0), (1, 1), (0, 0)))
    nb = 0.5 * (up[:, :-2] + up[:, 2:])
    return u + mu * (nb - u)


def depthwise_conv_centred(u, w, b):
    t = u.shape[1]
    up = jnp.pad(u, ((0, 0), (CONV_LEFT, CONV_W - 1 - CONV_LEFT), (0, 0)))
    return b + sum(w[j] * up[:, j:j + t] for j in range(CONV_W))


def wkv7_scan(r, w, k, v, kk, a, s0, reverse):
    def step(s, inp):
        r_t, w_t, k_t, v_t, kk_t, a_t = inp
        sa = jnp.einsum('bhvk,bhk->bhv', s, -kk_t)
        s = s * w_t[:, :, None, :] + sa[..., None] * (kk_t * a_t)[:, :, None, :] + v_t[..., None] * k_t[:, :, None, :]
        return s, jnp.einsum('bhvk,bhk->bhv', s, r_t)
    xs = tuple(jnp.moveaxis(t, 1, 0) for t in (r, w, k, v, kk, a))
    s_fin, y = lax.scan(step, s0, xs, reverse=reverse)
    return jnp.moveaxis(y, 0, 1), s_fin


def rwkv7_mixer(feats, s0, mu, w0, w_up, a0, a_up, g_up, k_k, k_a, r_k, gn_g, gn_b):
    f = centred_shift_mix(feats.astype(F32), mu)
    r, k, v, wd_f, wd_b, ad_f, ad_b, gd = jnp.split(f, RWKV_SPLITS, axis=-1)
    bsz, t, _ = r.shape
    heads = lambda z: z.reshape(bsz, t, H_A, HEAD_A)
    g = jax.nn.sigmoid(gd) @ g_up
    kk = heads(k * k_k)
    kk = kk / jnp.maximum(jnp.sqrt(jnp.sum(kk * kk, axis=-1, keepdims=True)), 1e-12)
    rh, vh = heads(r), heads(v)
    s0 = s0.astype(F32)
    y_sum = 0.0
    bonus = 0.0
    finals = []
    for d, (wd, ad) in enumerate(((wd_f, ad_f), (wd_b, ad_b))):
        w = jnp.exp(-DECAY_SCALE * jax.nn.sigmoid(w0[d] + jnp.tanh(wd) @ w_up[d]))
        a = jax.nn.sigmoid(a0[d] + ad @ a_up[d])
        kd = heads(k * (1.0 + (a - 1.0) * k_a))
        y, s_fin = wkv7_scan(rh, heads(w), kd, vh, kk, heads(a), s0[:, d], reverse=(d == 1))
        y_sum = y_sum + y
        bonus = bonus + jnp.sum(rh * kd * r_k, axis=-1, keepdims=True)
        finals.append(s_fin)
    m = jnp.mean(y_sum, axis=-1, keepdims=True)
    var = jnp.mean(jnp.square(y_sum - m), axis=-1, keepdims=True)
    yn = ((y_sum - m) * lax.rsqrt(var + GN_EPS_A)).reshape(bsz, t, D_A) * gn_g + gn_b
    out = (yn + (bonus * vh).reshape(bsz, t, D_A)) * g
    return out, jnp.stack(finals, axis=1)


def linear_recurrence(a, u, h0, reverse):
    def combine(left, right):
        a1, b1 = left
        a2, b2 = right
        return a1 * a2, a2 * b1 + b2
    a_cum, h = lax.associative_scan(combine, (a, u), reverse=reverse, axis=1)
    h = h + a_cum * h0[:, None]
    h_fin = h[:, 0] if reverse else h[:, -1]
    return h, h_fin


def rglru_mixer(xb, gb, h0, conv_w, conv_b, w_a, b_a, w_x, b_x, lam):
    xc = depthwise_conv_centred(xb.astype(F32), conv_w, conv_b)
    bsz, t, _ = xc.shape
    xh = xc.reshape(bsz, t, NB_B, LRU_BLOCK)
    h0 = h0.astype(F32)
    h_sum = 0.0
    finals = []
    for d in range(2):
        rg = jax.nn.sigmoid(jnp.einsum('btni,nij->btnj', xh, w_a[d]).reshape(bsz, t, D_B) + b_a[d])
        ig = jax.nn.sigmoid(jnp.einsum('btni,nij->btnj', xh, w_x[d]).reshape(bsz, t, D_B) + b_x[d])
        log_a = -LRU_C * rg * jax.nn.softplus(-lam[d])
        u = jnp.sqrt(-jnp.expm1(2.0 * log_a)) * (ig * xc)
        h, h_fin = linear_recurrence(jnp.exp(log_a), u, h0[:, d], reverse=(d == 1))
        h_sum = h_sum + h
        finals.append(h_fin)
    return h_sum * jax.nn.gelu(gb.astype(F32)), jnp.stack(finals, axis=1)


def ssd_chunked(x, dt, a, bh, ch, s0):
    bsz, t, h, p = x.shape
    nc = t // CHUNK
    xd = (x * dt[..., None]).reshape(bsz, nc, CHUNK, h, p)
    acs = jnp.cumsum(jnp.transpose((a * dt).reshape(bsz, nc, CHUNK, h), (0, 3, 1, 2)), axis=-1)
    bc = bh.reshape(bsz, nc, CHUNK, h, -1)
    cc = ch.reshape(bsz, nc, CHUNK, h, -1)
    causal = jnp.tril(jnp.ones((CHUNK, CHUNK), dtype=bool))
    lmat = jnp.exp(jnp.where(causal, acs[..., :, None] - acs[..., None, :], -jnp.inf))
    cb = jnp.einsum('bclhn,bcshn->bhcls', cc, bc)
    y_diag = jnp.einsum('bhcls,bcshp->bclhp', cb * lmat, xd)
    decay_states = jnp.exp(acs[..., -1:] - acs)
    chunk_states = jnp.einsum('bclhn,bhcl,bclhp->bchpn', bc, decay_states, xd)
    chunk_decay = jnp.exp(acs[..., -1])
    def step(s, inp):
        st, dec = inp
        return dec[..., None, None] * s + st, s
    s_fin, s_prev = lax.scan(step, s0, (jnp.moveaxis(chunk_states, 1, 0), jnp.moveaxis(chunk_decay, 2, 0)))
    s_prev = jnp.moveaxis(s_prev, 0, 1)
    y_off = jnp.einsum('bclhn,bchpn,bhcl->bclhp', cc, s_prev, jnp.exp(acs))
    return (y_diag + y_off).reshape(bsz, t, h, p), s_fin


def ssd_mixer(z, xbc, dt_f, dt_b, s0, conv_w, conv_b, dt_bias, a_log, d_skip, norm_g):
    xbc = jax.nn.silu(depthwise_conv_centred(xbc.astype(F32), conv_w, conv_b))
    xs, bm, cm = jnp.split(xbc, [D_C, D_C + G_C * N_C], axis=-1)
    bsz, t, _ = xs.shape
    xh = xs.reshape(bsz, t, H_C, P_C)
    bh = jnp.repeat(bm.reshape(bsz, t, G_C, N_C), H_C // G_C, axis=2)
    chh = jnp.repeat(cm.reshape(bsz, t, G_C, N_C), H_C // G_C, axis=2)
    s0 = s0.astype(F32)
    flip = lambda u: jnp.flip(u, axis=1)
    ident = lambda u: u
    y = d_skip[:, None] * xh
    finals = []
    for d, dt_raw in enumerate((dt_f, dt_b)):
        orient = ident if d == 0 else flip
        dt = jax.nn.softplus(dt_raw.astype(F32) + dt_bias[d])
        a = -jnp.exp(a_log[d].astype(F32))
        yd, s_fin = ssd_chunked(orient(xh), orient(dt), a, orient(bh), orient(chh), s0[:, d])
        y = y + orient(yd)
        finals.append(s_fin)
    yg = y.reshape(bsz, t, D_C) * jax.nn.silu(z.astype(F32))
    yn = yg * lax.rsqrt(jnp.mean(yg * yg, axis=-1, keepdims=True) + RMS_EPS) * norm_g
    return yn, jnp.stack(finals, axis=1)


def ec_moe(h, router_w, w_gate, w_up, w_down):
    bsz, t, d = h.shape
    cap = CAP_FACTOR * t // N_EXPERTS
    aff = jax.nn.softmax((h @ router_w).astype(F32), axis=-1)
    gates, idx = lax.top_k(jnp.swapaxes(aff, 1, 2), cap)
    xs = jax.vmap(lambda hb, ib: hb[ib])(h, idx)
    act = jax.nn.silu(jnp.einsum('becd,edf->becf', xs, w_gate)) * jnp.einsum('becd,edf->becf', xs, w_up)
    ye = jnp.einsum('becf,efd->becd', act, w_down) * gates[..., None].astype(h.dtype)
    return jax.vmap(lambda yb, ib: jnp.zeros((t, d), yb.dtype).at[ib.reshape(-1)].add(yb.reshape(-1, d)))(ye, idx)


def setup_inputs(seed: int = 0) -> dict:
    key = jax.random.key(seed)
    ks = iter(jax.random.split(key, 64))
    nrm = lambda shape, scale: scale * jax.random.normal(next(ks), shape, F32)
    unif = lambda shape, lo, hi: jax.random.uniform(next(ks), shape, F32, lo, hi)
    L = DEPTH
    a_pow = unif((L, 2, D_B), 0.9, 0.999)
    a_base = a_pow ** (1.0 / LRU_C)
    dt0 = jnp.exp(unif((L, 2, H_C), float(np.log(1e-3)), float(np.log(1e-1))))
    return {
        'x_prompt': nrm((BATCH, SEQ, D_MODEL), 1.0),
        'x_sample': nrm((DEC_BATCH, DEC_SEQ, D_MODEL), 1.0),
        'c': nrm((DEC_BATCH, D_MODEL), 1.0),
        'c_ctx': nrm((D_MODEL,), 1.0),
        'state_rwkv': nrm((DEC_BATCH, L, 2, H_A, HEAD_A, HEAD_A), 0.1),
        'state_rglru': nrm((DEC_BATCH, L, 2, D_B), 1.0),
        'state_ssd': nrm((DEC_BATCH, L, 2, H_C, P_C, N_C), 0.1),
        'w_mod': nrm((L, D_MODEL, 6 * D_MODEL), 0.5 * D_MODEL ** -0.5),
        'b_mod': nrm((L, 6 * D_MODEL), 0.02),
        'w_in': nrm((L, D_MODEL, N_IN), D_MODEL ** -0.5),
        'w_out': nrm((L, D_MIX, D_MODEL), BETA * D_MIX ** -0.5),
        'rwkv_mu': unif((L, RWKV_FEAT), 0.0, 1.0),
        'rwkv_w0': nrm((L, 2, D_A), 1.0),
        'rwkv_w_up': nrm((L, 2, R_W, D_A), R_W ** -0.5),
        'rwkv_a0': nrm((L, 2, D_A), 0.5),
        'rwkv_a_up': nrm((L, 2, R_A, D_A), R_A ** -0.5),
        'rwkv_g_up': nrm((L, R_G, D_A), R_G ** -0.5),
        'rwkv_k_k': 0.85 + nrm((L, D_A), 0.05),
        'rwkv_k_a': 1.0 + nrm((L, D_A), 0.05),
        'rwkv_r_k': nrm((L, H_A, HEAD_A), 0.1),
        'rwkv_gn_g': 1.0 + nrm((L, D_A), 0.05),
        'rwkv_gn_b': nrm((L, D_A), 0.02),
        'lru_conv_w': nrm((L, CONV_W, D_B), CONV_W ** -0.5),
        'lru_conv_b': nrm((L, D_B), 0.02),
        'lru_w_a': nrm((L, 2, NB_B, LRU_BLOCK, LRU_BLOCK), LRU_BLOCK ** -0.5),
        'lru_b_a': nrm((L, 2, D_B), 0.02),
        'lru_w_x': nrm((L, 2, NB_B, LRU_BLOCK, LRU_BLOCK), LRU_BLOCK ** -0.5),
        'lru_b_x': nrm((L, 2, D_B), 0.02),
        'lru_lambda': jnp.log(a_base) - jnp.log1p(-a_base),
        'ssd_conv_w': nrm((L, CONV_W, SSD_CONV_CH), CONV_W ** -0.5),
        'ssd_conv_b': nrm((L, SSD_CONV_CH), 0.02),
        'ssd_dt_bias': dt0 + jnp.log(-jnp.expm1(-dt0)),
        'ssd_a_log': jnp.log(unif((L, 2, H_C), 1.0, 16.0)),
        'ssd_d': 1.0 + nrm((L, H_C), 0.05),
        'ssd_norm_g': 1.0 + nrm((L, D_C), 0.05),
        'ln1_g': 1.0 + nrm((L, D_MODEL), 0.05),
        'ln1_b': nrm((L, D_MODEL), 0.02),
        'ln2_g': 1.0 + nrm((L, D_MODEL), 0.05),
        'ln2_b': nrm((L, D_MODEL), 0.02),
        'router_w': nrm((L, D_MODEL, N_EXPERTS), D_MODEL ** -0.5),
        'exp_w_gate': nrm((L, N_EXPERTS, D_MODEL, D_EXPERT), D_MODEL ** -0.5),
        'exp_w_up': nrm((L, N_EXPERTS, D_MODEL, D_EXPERT), D_MODEL ** -0.5),
        'exp_w_down': nrm((L, N_EXPERTS, D_EXPERT, D_MODEL), BETA * D_EXPERT ** -0.5),
    }


def reference(x_prompt, x_sample, c, c_ctx, state_rwkv, state_rglru, state_ssd,
              w_mod, b_mod, w_in, w_out,
              rwkv_mu, rwkv_w0, rwkv_w_up, rwkv_a0, rwkv_a_up, rwkv_g_up, rwkv_k_k, rwkv_k_a, rwkv_r_k,
              rwkv_gn_g, rwkv_gn_b,
              lru_conv_w, lru_conv_b, lru_w_a, lru_b_a, lru_w_x, lru_b_x, lru_lambda,
              ssd_conv_w, ssd_conv_b, ssd_dt_bias, ssd_a_log, ssd_d, ssd_norm_g,
              ln1_g, ln1_b, ln2_g, ln2_b,
              router_w, exp_w_gate, exp_w_up, exp_w_down):

    def layer(l, x, cond, s_rwkv, s_lru, s_ssd):
        mod = jax.nn.silu(cond) @ w_mod[l] + b_mod[l]
        sh1, sc1, gt1, sh2, sc2, gt2 = jnp.split(mod[:, None, :], 6, axis=-1)
        h = x * (1 + sc1) + sh1
        f_a, xb, gb, z, xbc, dt_f, dt_b = jnp.split(h @ w_in[l], IN_SPLITS, axis=-1)
        y_a, n_a = rwkv7_mixer(f_a, s_rwkv, rwkv_mu[l], rwkv_w0[l], rwkv_w_up[l], rwkv_a0[l], rwkv_a_up[l],
                               rwkv_g_up[l], rwkv_k_k[l], rwkv_k_a[l], rwkv_r_k[l], rwkv_gn_g[l], rwkv_gn_b[l])
        y_b, n_b = rglru_mixer(xb, gb, s_lru, lru_conv_w[l], lru_conv_b[l], lru_w_a[l], lru_b_a[l],
                               lru_w_x[l], lru_b_x[l], lru_lambda[l])
        y_c, n_c = ssd_mixer(z, xbc, dt_f, dt_b, s_ssd, ssd_conv_w[l], ssd_conv_b[l], ssd_dt_bias[l],
                             ssd_a_log[l], ssd_d[l], ssd_norm_g[l])
        o = jnp.concatenate([y_a, y_b, y_c], axis=-1).astype(x.dtype) @ w_out[l]
        x = layer_norm(ALPHA * x + gt1 * o, ln1_g[l], ln1_b[l])
        h = x * (1 + sc2) + sh2
        m = ec_moe(h, router_w[l], exp_w_gate[l], exp_w_up[l], exp_w_down[l])
        x = layer_norm(ALPHA * x + gt2 * m, ln2_g[l], ln2_b[l])
        return x, n_a, n_b, n_c

    nb = x_prompt.shape[0]
    y_prompt = x_prompt
    new_a, new_b, new_c = [], [], []
    for l in range(DEPTH):
        z_a = jnp.zeros((nb, 2, H_A, HEAD_A, HEAD_A), F32)
        z_b = jnp.zeros((nb, 2, D_B), F32)
        z_c = jnp.zeros((nb, 2, H_C, P_C, N_C), F32)
        y_prompt, s_a, s_b, s_c = layer(l, y_prompt, c_ctx[None, :], z_a, z_b, z_c)
        new_a.append(s_a)
        new_b.append(s_b)
        new_c.append(s_c)
    new_state_rwkv = jnp.stack(new_a, axis=1).astype(x_prompt.dtype)
    new_state_rglru = jnp.stack(new_b, axis=1).astype(x_prompt.dtype)
    new_state_ssd = jnp.stack(new_c, axis=1).astype(x_prompt.dtype)

    y_sample = x_sample + grid_pos_embed(x_sample.shape[1], x_sample.dtype)[None]
    for l in range(DEPTH):
        y_sample, _, _, _ = layer(l, y_sample, c, state_rwkv[:, l], state_rglru[:, l], state_ssd[:, l])

    return (y_prompt, y_sample, new_state_rwkv, new_state_rglru, new_state_ssd)
```

```python
import functools
from typing import NamedTuple

import numpy as np
import jax
import jax.numpy as jnp
from jax import lax
from jax.experimental import pallas as pl
from jax.experimental.pallas import tpu as pltpu

F32 = jnp.float32
BF16 = jnp.bfloat16
HI = lax.Precision.HIGHEST

D_MODEL = 2048
DEPTH = 2
GRID_W = 64
HEAD_A = 64
D_A = 512
H_A = 8
R_W = 64
R_A = 64
R_G = 128
DECAY_SCALE = 0.606531
GN_EPS_A = 64e-5
D_B = 768
LRU_BLOCK = 64
LRU_C = 8.0
CONV_W = 4
D_C = 768
P_C = 64
H_C = 12
G_C = 2
N_C = 128
SSD_CHUNK = 128
N_EXPERTS = 16
CAP_FACTOR = 2
D_EXPERT = 1024
ALPHA = (2.0 * DEPTH) ** 0.25
LN_EPS = 1e-5
RMS_EPS = 1e-5
RWKV_FEAT = 1920
N_IN = 5528

LANES = 128
SUBLANES = 8
WKV_CHUNK = 64
VMEM_LIMIT = 56 * 1024 * 1024

C_RWKV = 0
C_DTF = 1920
C_LX = 2048
C_LG = 2816
C_B = 3584
C_Z = 3840
C_X = 4608
C_C = 5376
C_DTB = 5632
NP = 5760


class Cfg(NamedTuple):
    n_ctx: int
    t_ctx: int
    n_lat: int
    t_lat: int

    @property
    def seg(self):
        return self.t_ctx

    @property
    def group(self):
        return self.t_lat

    @property
    def n_rows(self):
        return self.n_ctx * self.t_ctx + self.n_lat * self.t_lat

    @property
    def n_tiles(self):
        return self.n_rows // self.seg

    @property
    def n_groups(self):
        return self.n_rows // self.group

    @property
    def n_seq(self):
        return self.n_ctx + self.n_lat

    @property
    def segs_per_group(self):
        return self.group // self.seg


def _check_cfg(cfg):
    assert cfg.t_lat == SUBLANES * cfg.t_ctx, "latent length must be 8 context lengths"
    assert cfg.n_ctx % SUBLANES == 0
    assert cfg.seg % SSD_CHUNK == 0 and (cfg.seg & (cfg.seg - 1)) == 0
    assert cfg.n_ctx * cfg.t_ctx == cfg.n_lat * cfg.t_lat


def _tile_meta(cfg):
    first, last, seq = [], [], []
    for s in range(cfg.n_ctx):
        first.append(1); last.append(1); seq.append(s)
    spg = cfg.segs_per_group
    for s in range(cfg.n_lat):
        for k in range(spg):
            first.append(int(k == 0)); last.append(int(k == spg - 1)); seq.append(cfg.n_ctx + s)
    as32 = lambda v: jnp.asarray(np.asarray(v, np.int32))
    return as32(first), as32(last), as32(seq)


def _group_isctx(cfg):
    n_cg = cfg.n_ctx // SUBLANES
    return jnp.asarray(np.asarray([1] * n_cg + [0] * cfg.n_lat, np.int32))


def _row_cond(cfg, tm):
    n_c = cfg.n_ctx * cfg.t_ctx
    assert n_c % tm == 0 and cfg.t_lat % tm == 0
    out = []
    for i in range(cfg.n_rows // tm):
        r = i * tm
        out.append(0 if r < n_c else 1 + (r - n_c) // cfg.t_lat)
    return jnp.asarray(np.asarray(out, np.int32))


def _cparams(sem):
    return pltpu.CompilerParams(dimension_semantics=sem, vmem_limit_bytes=VMEM_LIMIT)


def _dot(a, b, prec=None):
    return jnp.dot(a, b, preferred_element_type=F32, precision=prec)


def _dot_nt(a, b, prec=None):
    return lax.dot_general(a, b, (((1,), (1,)), ((), ())), preferred_element_type=F32, precision=prec)


def _dot_tn(a, b, prec=None):
    return lax.dot_general(a, b, (((0,), (0,)), ((), ())), preferred_element_type=F32, precision=prec)


def _sigmoid(x):
    return 1.0 / (1.0 + jnp.exp(-x))


def _silu(x):
    return x * _sigmoid(x)


def _softplus(x):
    return jnp.maximum(x, 0.0) + jnp.log1p(jnp.exp(-jnp.abs(x)))


def _expm1(y):
    u = jnp.exp(y)
    um1 = u - 1.0
    flat = um1 == 0.0
    val = um1 * y / jnp.where(flat, 1.0, jnp.log(jnp.where(u == 0.0, 1.0, u)))
    return jnp.where(flat, y, jnp.where(u == 0.0, -1.0, val))


def _layer_norm(x, g, b):
    mu = jnp.mean(x, axis=-1, keepdims=True)
    xc = x - mu
    var = jnp.mean(xc * xc, axis=-1, keepdims=True)
    return xc * lax.rsqrt(var + LN_EPS) * g + b


def _mod_kernel(c_ref, w_ref, b_ref, o_ref):
    o_ref[0] = _dot(_silu(c_ref[...]), w_ref[0], HI) + b_ref[0]


def _mod_call(cond8, w_mod, b_mod):
    depth, d, n6 = w_mod.shape
    tn = 1024
    return pl.pallas_call(
        _mod_kernel,
        grid=(depth, n6 // tn),
        in_specs=[pl.BlockSpec((SUBLANES, d), lambda l, j: (0, 0)),
                  pl.BlockSpec((1, d, tn), lambda l, j: (l, 0, j)),
                  pl.BlockSpec((1, 1, tn), lambda l, j: (l, 0, j))],
        out_specs=pl.BlockSpec((1, SUBLANES, tn), lambda l, j: (l, 0, j)),
        out_shape=jax.ShapeDtypeStruct((depth, SUBLANES, n6), F32),
        compiler_params=_cparams(("arbitrary", "arbitrary")),
        name="mod",
    )(cond8, w_mod, b_mod.reshape(depth, 1, n6))


def _win_kernel(rc_ref, x_ref, sc_ref, sh_ref, w_ref, o_ref, h_scr):
    @pl.when(pl.program_id(1) == 0)
    def _():
        h_scr[...] = (x_ref[...] * (1.0 + sc_ref[0]) + sh_ref[0]).astype(BF16)

    o_ref[...] = _dot(h_scr[...], w_ref[...].astype(BF16))


def _win_call(cfg, x, sc, sh, w_p):
    n = cfg.n_rows
    tm = min(512, cfg.seg * 2)
    tn = 640
    rc = _row_cond(cfg, tm)
    gs = pltpu.PrefetchScalarGridSpec(
        num_scalar_prefetch=1,
        grid=(n // tm, NP // tn),
        in_specs=[pl.BlockSpec((tm, D_MODEL), lambda i, j, rc: (i, 0)),
                  pl.BlockSpec((1, 1, D_MODEL), lambda i, j, rc: (rc[i], 0, 0)),
                  pl.BlockSpec((1, 1, D_MODEL), lambda i, j, rc: (rc[i], 0, 0)),
                  pl.BlockSpec((D_MODEL, tn), lambda i, j, rc: (0, j))],
        out_specs=pl.BlockSpec((tm, tn), lambda i, j, rc: (i, j)),
        scratch_shapes=[pltpu.VMEM((tm, D_MODEL), BF16)],
    )
    return pl.pallas_call(
        _win_kernel, grid_spec=gs,
        out_shape=jax.ShapeDtypeStruct((n, NP), F32),
        compiler_params=_cparams(("arbitrary", "arbitrary")),
        name="w_in",
    )(rc, x, sc, sh, w_p)


def _permute_w_in(w):
    d = w.shape[0]
    pad = jnp.zeros((d, LANES - H_C), w.dtype)
    parts = [w[:, 0:1920], w[:, 5504:5516], pad,
             w[:, 1920:2688], w[:, 2688:3456],
             w[:, 4992:5248], w[:, 3456:4224], w[:, 4224:4992], w[:, 5248:5504],
             w[:, 5516:5528], pad]
    return jnp.concatenate(parts, axis=1)


def _shifted_rows(x, prev8, next8, shifts):
    seg = x.shape[0]
    ext = jnp.concatenate([prev8, x, next8], axis=0)
    rows = ext.shape[0]
    out = []
    for s in shifts:
        rolled = pltpu.roll(ext, s % rows, axis=0)
        out.append(rolled[SUBLANES:SUBLANES + seg])
    return out


def _rwkv_prep_kernel(first_ref, last_ref, f_ref, prev_ref, next_ref, mu_ref, w0_ref, wup_ref, a0_ref,
                      aup_ref, gup_ref, kk_ref, ka_ref, rk_ref, bd_ref,
                      r_out, v_out, kkn_out, g_out, wl_out, kd_out, a_out, bonus_out):
    i = pl.program_id(0)
    u = f_ref[...]
    pv = jnp.where(first_ref[i] == 1, 0.0, prev_ref[...])
    nx = jnp.where(last_ref[i] == 1, 0.0, next_ref[...])
    up, dn = _shifted_rows(u, pv, nx, (1, -1))
    f = u + mu_ref[...] * (0.5 * (up + dn) - u)
    r = f[:, 0:512]
    k = f[:, 512:1024]
    v = f[:, 1024:1536]
    tw = jnp.tanh(f[:, 1536:1664])
    ad = f[:, 1664:1792]
    gd = f[:, 1792:1920]
    bd = bd_ref[...]
    g = _dot(_sigmoid(gd), gup_ref[...], HI)
    kk = k * kk_ref[...]
    ss = _dot(kk * kk, bd, HI)
    kk = kk / jnp.maximum(jnp.sqrt(ss), 1e-12)
    r_out[...] = r
    v_out[...] = v
    kkn_out[...] = kk
    g_out[...] = g
    kd_sum = jnp.zeros_like(k)
    for d in range(2):
        wl = -DECAY_SCALE * _sigmoid(w0_ref[d:d + 1, :] + _dot(tw, wup_ref[d], HI))
        a = _sigmoid(a0_ref[d:d + 1, :] + _dot(ad, aup_ref[d], HI))
        kd = k * (1.0 + (a - 1.0) * ka_ref[...])
        wl_out[d] = wl
        a_out[d] = a
        kd_out[d] = kd
        kd_sum = kd_sum + kd
    bonus_out[...] = _dot(r * kd_sum * rk_ref[...], bd, HI)


def _rwkv_prep_call(cfg, p, meta, prm):
    n, seg = cfg.n_rows, cfg.seg
    first, last, _ = meta
    nb8 = seg // SUBLANES
    last8 = n // SUBLANES - 1
    full = lambda shape: pl.BlockSpec(shape, lambda i, *_: (0,) * len(shape))
    tile = pl.BlockSpec((seg, D_A), lambda i, *_: (i, 0))
    tile2 = pl.BlockSpec((2, seg, D_A), lambda i, *_: (0, i, 0))
    gs = pltpu.PrefetchScalarGridSpec(
        num_scalar_prefetch=2,
        grid=(cfg.n_tiles,),
        in_specs=[pl.BlockSpec((seg, RWKV_FEAT), lambda i, *_: (i, 0)),
                  pl.BlockSpec((SUBLANES, RWKV_FEAT), lambda i, *_: (jnp.maximum(i * nb8 - 1, 0), 0)),
                  pl.BlockSpec((SUBLANES, RWKV_FEAT), lambda i, *_: (jnp.minimum((i + 1) * nb8, last8), 0)),
                  full((1, RWKV_FEAT)), full((2, D_A)), full((2, LANES, D_A)), full((2, D_A)),
                  full((2, LANES, D_A)), full((R_G, D_A)), full((1, D_A)), full((1, D_A)), full((1, D_A)),
                  full((D_A, D_A))],
        out_specs=[tile, tile, tile, tile, tile2, tile2, tile2, tile],
    )
    sd = jax.ShapeDtypeStruct
    return pl.pallas_call(
        _rwkv_prep_kernel, grid_spec=gs,
        out_shape=[sd((n, D_A), F32)] * 4 + [sd((2, n, D_A), F32)] * 3 + [sd((n, D_A), F32)],
        compiler_params=_cparams(("arbitrary",)),
        name="rwkv_prep",
    )(first, last, p, p, p, prm["mu"], prm["w0"], prm["wup"], prm["a0"], prm["aup"], prm["gup"],
      prm["k_k"], prm["k_a"], prm["r_k"], prm["bd"])


def _wkv_scan_kernel(first_ref, last_ref, seq_ref, r_ref, v_ref, kk_ref, wl_ref, kd_ref, a_ref, s0_ref,
                     y_ref, sfin_ref, s_scr, *, n_tiles, seg):
    d = pl.program_id(0)
    i = pl.program_id(1)
    t = i + d * (n_tiles - 1 - 2 * i)
    is_start = jnp.where(d == 0, first_ref[t], last_ref[t]) == 1
    is_end = jnp.where(d == 0, last_ref[t], first_ref[t]) == 1
    sgn = 1 - 2 * d
    c = WKV_CHUNK
    n_hp = H_A // 2

    @pl.when(is_start)
    def _():
        s_scr[...] = s0_ref[...]

    ri = lax.broadcasted_iota(jnp.int32, (c, c), 0)
    ci = lax.broadcasted_iota(jnp.int32, (c, c), 1)
    incl64 = ((ri - ci) * sgn >= 0).astype(F32)
    r2 = lax.broadcasted_iota(jnp.int32, (2 * c, 2 * c), 0)
    c2 = lax.broadcasted_iota(jnp.int32, (2 * c, 2 * c), 1)
    same = (r2 // c) == (c2 // c)
    dt2 = ((r2 % c) - (c2 % c)) * sgn
    strict = same & (dt2 > 0)
    incl = same & (dt2 >= 0)
    lane = lax.broadcasted_iota(jnp.int32, (1, LANES), 1)
    m_a = lane < HEAD_A

    def stack(x):
        return jnp.concatenate([jnp.where(m_a, x, 0.0), jnp.where(m_a, 0.0, x)], axis=0)

    def chunk(ck, carry):
        cc = ck + d * (seg // c - 1 - 2 * ck)
        rows = pl.ds(pl.multiple_of(cc * c, c), c)
        for hp in range(n_hp):
            lanes = slice(hp * LANES, (hp + 1) * LANES)
            r = r_ref[rows, lanes]
            v = v_ref[rows, lanes]
            kk = kk_ref[rows, lanes]
            wl = wl_ref[rows, lanes]
            kd = kd_ref[rows, lanes]
            a = a_ref[rows, lanes]
            cum = _dot(incl64, wl, HI)
            tot = jnp.sum(wl, axis=0, keepdims=True)
            e_in = jnp.exp(cum)
            e_ex = jnp.exp(cum - wl)
            e_ng = jnp.exp(-cum)
            e_rm = jnp.exp(tot - cum)
            beta = kk * a
            lhs = jnp.concatenate([stack(-kk * e_ex), stack(r * e_in)], axis=0)
            rhs = jnp.concatenate([stack(beta * e_ng), stack(kd * e_ng)], axis=0)
            ab = _dot_nt(lhs, rhs, HI)
            a_m = jnp.where(strict, ab[0:2 * c, 0:2 * c], 0.0)
            b_m = jnp.where(strict, ab[0:2 * c, 2 * c:4 * c], 0.0)
            a_i = jnp.where(incl, ab[2 * c:4 * c, 0:2 * c], 0.0)
            b_i = jnp.where(incl, ab[2 * c:4 * c, 2 * c:4 * c], 0.0)
            s = s_scr[hp]
            ls = _dot_nt(lhs, s, HI)
            vs = stack(v)
            x = ls[0:2 * c] + _dot(b_m, vs, HI)
            m = a_m
            for j in range(6):
                x = x + _dot(m, x, HI)
                if j < 5:
                    m = _dot(m, m, HI)
            ys = ls[2 * c:4 * c] + _dot(a_i, x, HI) + _dot(b_i, vs, HI)
            y_ref[rows, lanes] = ys[0:c] + ys[c:2 * c]
            s_scr[hp] = (s * jnp.exp(tot) + _dot_tn(x, stack(beta * e_rm), HI)
                         + _dot_tn(vs, stack(kd * e_rm), HI))
        return carry

    lax.fori_loop(0, seg // c, chunk, 0)

    @pl.when(is_end)
    def _():
        sfin_ref[...] = s_scr[...]


def _wkv_scan_call(cfg, meta, r, v, kk, wl, kd, a, s0):
    n, seg, nt = cfg.n_rows, cfg.seg, cfg.n_tiles
    first, last, seq = meta
    n_hp = H_A // 2
    tmap = lambda d, i: i + d * (nt - 1 - 2 * i)
    tile = pl.BlockSpec((seg, D_A), lambda d, i, *_: (tmap(d, i), 0))
    tile_d = pl.BlockSpec((None, seg, D_A), lambda d, i, *_: (d, tmap(d, i), 0))
    st = pl.BlockSpec((None, None, n_hp, LANES, LANES),
                      lambda d, i, f, l, sq: (d, sq[tmap(d, i)], 0, 0, 0))
    gs = pltpu.PrefetchScalarGridSpec(
        num_scalar_prefetch=3,
        grid=(2, nt),
        in_specs=[tile, tile, tile, tile_d, tile_d, tile_d, st],
        out_specs=[tile_d, st],
        scratch_shapes=[pltpu.VMEM((n_hp, LANES, LANES), F32)],
    )
    sd = jax.ShapeDtypeStruct
    return pl.pallas_call(
        functools.partial(_wkv_scan_kernel, n_tiles=nt, seg=seg), grid_spec=gs,
        out_shape=[sd((2, n, D_A), F32), sd((2, cfg.n_seq, n_hp, LANES, LANES), F32)],
        compiler_params=_cparams(("arbitrary", "arbitrary")),
        name="wkv_scan",
    )(first, last, seq, r, v, kk, wl, kd, a, s0)


def _rwkv_post_kernel(y_ref, v_ref, g_ref, bonus_ref, gng_ref, gnb_ref, bd_ref, o_ref):
    y = y_ref[0] + y_ref[1]
    bd = bd_ref[...]
    m = _dot(y, bd, HI) * (1.0 / HEAD_A)
    yc = y - m
    var = _dot(yc * yc, bd, HI) * (1.0 / HEAD_A)
    yn = yc * lax.rsqrt(var + GN_EPS_A) * gng_ref[...] + gnb_ref[...]
    o_ref[...] = (yn + bonus_ref[...] * v_ref[...]) * g_ref[...]


def _rwkv_post_call(cfg, y2, v, g, bonus, prm):
    n, seg = cfg.n_rows, cfg.seg
    tile = pl.BlockSpec((seg, D_A), lambda i: (i, 0))
    full = lambda shape: pl.BlockSpec(shape, lambda i: (0,) * len(shape))
    return pl.pallas_call(
        _rwkv_post_kernel, grid=(cfg.n_tiles,),
        in_specs=[pl.BlockSpec((2, seg, D_A), lambda i: (0, i, 0)), tile, tile, tile,
                  full((1, D_A)), full((1, D_A)), full((D_A, D_A))],
        out_specs=tile,
        out_shape=jax.ShapeDtypeStruct((n, D_A), F32),
        compiler_params=_cparams(("arbitrary",)),
        name="rwkv_post",
    )(y2, v, g, bonus, prm["gn_g"], prm["gn_b"], prm["bd"])


def _lru_kernel(isctx_ref, xb_ref, gb_ref, cw_ref, cb_ref, wg_ref, bg_ref, lam_ref, h0_ref,
                y_ref, hfin_ref, af_s, uf_s, ab_s, ub_s, *, seg, group):
    gi = pl.program_id(0)
    ctx = isctx_ref[gi] == 1
    nseg = group // seg
    x = xb_ref[...]
    row = lax.broadcasted_iota(jnp.int32, (group, 1), 0)
    tl = row & (seg - 1)
    lo = jnp.logical_or(ctx, row < seg)
    hi = jnp.logical_or(ctx, row >= group - seg)
    xm2 = jnp.where(jnp.logical_and(tl < 2, lo), 0.0, pltpu.roll(x, 2, axis=0))
    xm1 = jnp.where(jnp.logical_and(tl < 1, lo), 0.0, pltpu.roll(x, 1, axis=0))
    xp1 = jnp.where(jnp.logical_and(tl == seg - 1, hi), 0.0, pltpu.roll(x, group - 1, axis=0))
    xc = (cb_ref[...] + cw_ref[0:1, :] * xm2 + cw_ref[1:2, :] * xm1 + cw_ref[2:3, :] * x
          + cw_ref[3:4, :] * xp1)
    gates = _dot(xc, wg_ref[0], HI) + bg_ref[0]
    for d, (a_s, u_s) in enumerate(((af_s, uf_s), (ab_s, ub_s))):
        rg = _sigmoid(gates[:, (2 * d) * LANES:(2 * d + 1) * LANES])
        ig = _sigmoid(gates[:, (2 * d + 1) * LANES:(2 * d + 2) * LANES])
        log_a = -LRU_C * rg * _softplus(-lam_ref[d:d + 1, :])
        a_s[...] = jnp.exp(log_a)
        u_s[...] = jnp.sqrt(-_expm1(2.0 * log_a)) * (ig * xc)

    def step(t, carry):
        hf, pf, hb, pb = carry
        sf = pl.ds(t, nseg, stride=seg)
        a = af_s[sf, :]
        hf = a * hf + uf_s[sf, :]
        pf = a * pf
        uf_s[sf, :] = hf
        af_s[sf, :] = pf
        sb = pl.ds(seg - 1 - t, nseg, stride=seg)
        a = ab_s[sb, :]
        hb = a * hb + ub_s[sb, :]
        pb = a * pb
        ub_s[sb, :] = hb
        ab_s[sb, :] = pb
        return hf, pf, hb, pb

    z = jnp.zeros((nseg, LANES), F32)
    o = jnp.ones((nseg, LANES), F32)
    hf, pf, hb, pb = lax.fori_loop(0, seg, step, (z, o, z, o), unroll=8)

    h0f = h0_ref[0, 0]
    h0b = h0_ref[0, 1]
    cf = [h0f[0:1]]
    for s in range(nseg - 1):
        cf.append(hf[s:s + 1] + pf[s:s + 1] * cf[s])
    cbk = [h0b[0:1]]
    for s in range(nseg - 1, 0, -1):
        cbk.append(hb[s:s + 1] + pb[s:s + 1] * cbk[-1])
    cbk = cbk[::-1]
    cin_f = jnp.where(ctx, h0f, jnp.concatenate(cf, axis=0))
    cin_b = jnp.where(ctx, h0b, jnp.concatenate(cbk, axis=0))
    hfin_ref[0, 0] = hf + pf * cin_f
    hfin_ref[0, 1] = hb + pb * cin_b
    for s in range(nseg):
        rows = slice(s * seg, (s + 1) * seg)
        h = (uf_s[rows, :] + af_s[rows, :] * cin_f[s:s + 1] + ub_s[rows, :] + ab_s[rows, :] * cin_b[s:s + 1])
        y_ref[rows, :] = h * jax.nn.gelu(gb_ref[rows, :])


def _lru_call(cfg, p, prm, h0):
    n, seg, group = cfg.n_rows, cfg.seg, cfg.group
    ncg = D_B // LANES
    nseg = cfg.segs_per_group
    isctx = _group_isctx(cfg)
    gs = pltpu.PrefetchScalarGridSpec(
        num_scalar_prefetch=1,
        grid=(cfg.n_groups, ncg),
        in_specs=[pl.BlockSpec((group, LANES), lambda g, c, *_: (g, C_LX // LANES + c)),
                  pl.BlockSpec((group, LANES), lambda g, c, *_: (g, C_LG // LANES + c)),
                  pl.BlockSpec((CONV_W, LANES), lambda g, c, *_: (0, c)),
                  pl.BlockSpec((1, LANES), lambda g, c, *_: (0, c)),
                  pl.BlockSpec((1, LANES, 4 * LANES), lambda g, c, *_: (c, 0, 0)),
                  pl.BlockSpec((1, 1, 4 * LANES), lambda g, c, *_: (c, 0, 0)),
                  pl.BlockSpec((2, LANES), lambda g, c, *_: (0, c)),
                  pl.BlockSpec((1, 2, nseg, LANES), lambda g, c, *_: (g, 0, 0, c))],
        out_specs=[pl.BlockSpec((group, LANES), lambda g, c, *_: (g, c)),
                   pl.BlockSpec((1, 2, nseg, LANES), lambda g, c, *_: (g, 0, 0, c))],
        scratch_shapes=[pltpu.VMEM((group, LANES), F32)] * 4,
    )
    sd = jax.ShapeDtypeStruct
    return pl.pallas_call(
        functools.partial(_lru_kernel, seg=seg, group=group), grid_spec=gs,
        out_shape=[sd((n, D_B), F32), sd((cfg.n_groups, 2, nseg, D_B), F32)],
        compiler_params=_cparams(("arbitrary", "arbitrary")),
        name="lru",
    )(isctx, p, p, prm["conv_w"], prm["conv_b"], prm["wg"], prm["bg"], prm["lam"], h0)


def _ssd_prep_kernel(first_ref, last_ref, x_ref, xp_ref, xn_ref, b_ref, bp_ref, bn_ref, c_ref, cp_ref, cn_ref,
                     dtf_ref, dtb_ref, cwx_ref, cbx_ref, cwb_ref, cbb_ref, cwc_ref, cbc_ref, dtbias_ref,
                     xo_ref, bo_ref, co_ref, dto_ref):
    i = pl.program_id(0)
    is_first = first_ref[i] == 1
    is_last = last_ref[i] == 1

    def conv(cur, prev, nxt, cw, cb):
        u = cur[...]
        pv = jnp.where(is_first, 0.0, prev[...])
        nx = jnp.where(is_last, 0.0, nxt[...])
        m2, m1, p1 = _shifted_rows(u, pv, nx, (2, 1, -1))
        y = cb[...] + cw[0:1, :] * m2 + cw[1:2, :] * m1 + cw[2:3, :] * u + cw[3:4, :] * p1
        return _silu(y)

    xo_ref[...] = conv(x_ref, xp_ref, xn_ref, cwx_ref, cbx_ref)
    bo_ref[...] = conv(b_ref, bp_ref, bn_ref, cwb_ref, cbb_ref)
    co_ref[...] = conv(c_ref, cp_ref, cn_ref, cwc_ref, cbc_ref)
    dto_ref[0] = _softplus(dtf_ref[...] + dtbias_ref[0:1, :])
    dto_ref[1] = _softplus(dtb_ref[...] + dtbias_ref[1:2, :])


def _ssd_prep_call(cfg, p, meta, prm):
    n, seg = cfg.n_rows, cfg.seg
    first, last, _ = meta
    nb8 = seg // SUBLANES
    last8 = n // SUBLANES - 1

    def trio(width, col):
        cb = col // width
        c8 = col // width
        return [pl.BlockSpec((seg, width), lambda i, *_: (i, cb)),
                pl.BlockSpec((SUBLANES, width), lambda i, *_: (jnp.maximum(i * nb8 - 1, 0), c8)),
                pl.BlockSpec((SUBLANES, width), lambda i, *_: (jnp.minimum((i + 1) * nb8, last8), c8))]

    full = lambda shape: pl.BlockSpec(shape, lambda i, *_: (0,) * len(shape))
    n_bc = G_C * N_C
    gs = pltpu.PrefetchScalarGridSpec(
        num_scalar_prefetch=2,
        grid=(cfg.n_tiles,),
        in_specs=(trio(D_C, C_X) + trio(n_bc, C_B) + trio(n_bc, C_C)
                  + [pl.BlockSpec((seg, LANES), lambda i, *_: (i, C_DTF // LANES)),
                     pl.BlockSpec((seg, LANES), lambda i, *_: (i, C_DTB // LANES)),
                     full((CONV_W, D_C)), full((1, D_C)), full((CONV_W, n_bc)), full((1, n_bc)),
                     full((CONV_W, n_bc)), full((1, n_bc)), full((2, LANES))]),
        out_specs=[pl.BlockSpec((seg, D_C), lambda i, *_: (i, 0)),
                   pl.BlockSpec((seg, n_bc), lambda i, *_: (i, 0)),
                   pl.BlockSpec((seg, n_bc), lambda i, *_: (i, 0)),
                   pl.BlockSpec((2, seg, LANES), lambda i, *_: (0, i, 0))],
    )
    sd = jax.ShapeDtypeStruct
    return pl.pallas_call(
        _ssd_prep_kernel, grid_spec=gs,
        out_shape=[sd((n, D_C), F32), sd((n, n_bc), F32), sd((n, n_bc), F32), sd((2, n, LANES), F32)],
        compiler_params=_cparams(("arbitrary",)),
        name="ssd_prep",
    )(first, last, p, p, p, p, p, p, p, p, p, p, p,
      prm["cwx"], prm["cbx"], prm["cwb"], prm["cbb"], prm["cwc"], prm["cbc"], prm["dt_bias"])


def _ssd_scan_kernel(first_ref, last_ref, seq_ref, x_ref, b_ref, c_ref, dt_ref, alog_ref, exp_ref, s0_ref,
                     y_ref, sfin_ref, s_scr, *, n_tiles, seg):
    d = pl.program_id(0)
    i = pl.program_id(1)
    t = i + d * (n_tiles - 1 - 2 * i)
    is_start = jnp.where(d == 0, first_ref[t], last_ref[t]) == 1
    is_end = jnp.where(d == 0, last_ref[t], first_ref[t]) == 1
    sgn = 1 - 2 * d
    L = SSD_CHUNK
    n_hp = H_C // 2
    hpg = n_hp // G_C

    @pl.when(is_start)
    def _():
        s_scr[...] = s0_ref[...]

    ri = lax.broadcasted_iota(jnp.int32, (L, L), 0)
    ci = lax.broadcasted_iota(jnp.int32, (L, L), 1)
    incl = (ri - ci) * sgn >= 0
    tri = incl.astype(F32)
    lane = lax.broadcasted_iota(jnp.int32, (1, LANES), 1)
    m_a = lane < P_C
    a_neg = -jnp.exp(alog_ref[0])
    expand = exp_ref[...]

    def chunk(ck, carry):
        cc = ck + d * (seg // L - 1 - 2 * ck)
        rows = pl.ds(pl.multiple_of(cc * L, L), L)
        x = x_ref[rows, :]
        bm = b_ref[rows, :]
        cm = c_ref[rows, :]
        dt = dt_ref[rows, :]
        dta = dt * a_neg
        acs = _dot(tri, dta, HI)
        tot = jnp.sum(dta, axis=0, keepdims=True)
        acs_t = _dot_nt(dta.T, tri, HI)
        acs_x = _dot(acs, expand, HI)
        tot_x = _dot(tot, expand, HI)
        xd = x * _dot(dt, expand, HI)
        e_in = jnp.exp(acs_x)
        xds = xd * jnp.exp(tot_x - acs_x)
        dec = jnp.exp(tot_x)
        for g in range(G_C):
            bg = bm[:, g * N_C:(g + 1) * N_C]
            cg = cm[:, g * N_C:(g + 1) * N_C]
            cb = _dot_nt(cg, bg)
            bg_t = bg.T
            for q in range(hpg):
                hp = g * hpg + q
                lanes = slice(hp * LANES, (hp + 1) * LANES)
                gs = []
                for h in (2 * hp, 2 * hp + 1):
                    col = acs[:, h:h + 1]
                    rowv = acs_t[h:h + 1, :]
                    gs.append(jnp.where(incl, cb * jnp.exp(col - rowv), 0.0))
                xdp = xd[:, lanes]
                rhs = jnp.concatenate([jnp.where(m_a, xdp, 0.0), jnp.where(m_a, 0.0, xdp)], axis=0)
                y_diag = _dot(jnp.concatenate(gs, axis=1), rhs)
                s = s_scr[hp]
                y_off = _dot(cg, s) * e_in[:, lanes]
                y_ref[rows, lanes] = y_diag + y_off
                s_scr[hp] = s * dec[:, lanes] + _dot(bg_t, xds[:, lanes])
        return carry

    lax.fori_loop(0, seg // L, chunk, 0)

    @pl.when(is_end)
    def _():
        sfin_ref[...] = s_scr[...]


def _ssd_scan_call(cfg, meta, xc, bc, cc, dt2, prm, s0):
    n, seg, nt = cfg.n_rows, cfg.seg, cfg.n_tiles
    first, last, seq = meta
    n_hp = H_C // 2
    n_bc = G_C * N_C
    tmap = lambda d, i: i + d * (nt - 1 - 2 * i)
    st = pl.BlockSpec((None, None, n_hp, N_C, LANES), lambda d, i, f, l, sq: (d, sq[tmap(d, i)], 0, 0, 0))
    gs = pltpu.PrefetchScalarGridSpec(
        num_scalar_prefetch=3,
        grid=(2, nt),
        in_specs=[pl.BlockSpec((seg, D_C), lambda d, i, *_: (tmap(d, i), 0)),
                  pl.BlockSpec((seg, n_bc), lambda d, i, *_: (tmap(d, i), 0)),
                  pl.BlockSpec((seg, n_bc), lambda d, i, *_: (tmap(d, i), 0)),
                  pl.BlockSpec((None, seg, LANES), lambda d, i, *_: (d, tmap(d, i), 0)),
                  pl.BlockSpec((1, 1, LANES), lambda d, i, *_: (d, 0, 0)),
                  pl.BlockSpec((LANES, D_C), lambda d, i, *_: (0, 0)),
                  st],
        out_specs=[pl.BlockSpec((None, seg, D_C), lambda d, i, *_: (d, tmap(d, i), 0)), st],
        scratch_shapes=[pltpu.VMEM((n_hp, N_C, LANES), F32)],
    )
    sd = jax.ShapeDtypeStruct
    return pl.pallas_call(
        functools.partial(_ssd_scan_kernel, n_tiles=nt, seg=seg), grid_spec=gs,
        out_shape=[sd((2, n, D_C), F32), sd((2, cfg.n_seq, n_hp, N_C, LANES), F32)],
        compiler_params=_cparams(("arbitrary", "arbitrary")),
        name="ssd_scan",
    )(first, last, seq, xc, bc, cc, dt2, prm["a_log"], prm["expand"], s0)


def _ssd_post_kernel(y_ref, x_ref, z_ref, dskip_ref, ng_ref, o_ref):
    y = dskip_ref[...] * x_ref[...] + y_ref[0] + y_ref[1]
    yg = y * _silu(z_ref[...])
    o_ref[...] = yg * lax.rsqrt(jnp.mean(yg * yg, axis=-1, keepdims=True) + RMS_EPS) * ng_ref[...]


def _ssd_post_call(cfg, y2, xc, p, prm):
    n, seg = cfg.n_rows, cfg.seg
    tile = pl.BlockSpec((seg, D_C), lambda i: (i, 0))
    full = lambda shape: pl.BlockSpec(shape, lambda i: (0,) * len(shape))
    return pl.pallas_call(
        _ssd_post_kernel, grid=(cfg.n_tiles,),
        in_specs=[pl.BlockSpec((2, seg, D_C), lambda i: (0, i, 0)), tile,
                  pl.BlockSpec((seg, D_C), lambda i: (i, C_Z // D_C)), full((1, D_C)), full((1, D_C))],
        out_specs=tile,
        out_shape=jax.ShapeDtypeStruct((n, D_C), F32),
        compiler_params=_cparams(("arbitrary",)),
        name="ssd_post",
    )(y2, xc, p, prm["d_skip"], prm["norm_g"])


def _wout_kernel(rc_ref, ya_ref, yb_ref, yc_ref, x_ref, w_ref, gt_ref, sc_ref, sh_ref, g_ref, b_ref, rw_ref,
                 x1_ref, h2_ref, lg_ref):
    o = (_dot(ya_ref[...].astype(BF16), w_ref[0:D_A, :])
         + _dot(yb_ref[...].astype(BF16), w_ref[D_A:D_A + D_B, :])
         + _dot(yc_ref[...].astype(BF16), w_ref[D_A + D_B:D_MODEL, :]))
    x1 = _layer_norm(ALPHA * x_ref[...] + gt_ref[0] * o, g_ref[...], b_ref[...])
    x1_ref[...] = x1
    h2 = x1 * (1.0 + sc_ref[0]) + sh_ref[0]
    h2_ref[...] = h2.astype(BF16)
    lg_ref[...] = _dot(h2, rw_ref[...], HI)


def _wout_call(cfg, ya, yb, yc, x, w_bf, gt, sc, sh, g, b, rw_pad):
    n = cfg.n_rows
    tm = cfg.seg
    rc = _row_cond(cfg, tm)
    row = lambda w: pl.BlockSpec((tm, w), lambda i, rc: (i, 0))
    cond = pl.BlockSpec((1, 1, D_MODEL), lambda i, rc: (rc[i], 0, 0))
    full = lambda shape: pl.BlockSpec(shape, lambda i, rc: (0,) * len(shape))
    gs = pltpu.PrefetchScalarGridSpec(
        num_scalar_prefetch=1,
        grid=(n // tm,),
        in_specs=[row(D_A), row(D_B), row(D_C), row(D_MODEL), full((D_MODEL, D_MODEL)), cond, cond, cond,
                  full((1, D_MODEL)), full((1, D_MODEL)), full((D_MODEL, LANES))],
        out_specs=[row(D_MODEL), row(D_MODEL), row(LANES)],
    )
    sd = jax.ShapeDtypeStruct
    return pl.pallas_call(
        _wout_kernel, grid_spec=gs,
        out_shape=[sd((n, D_MODEL), F32), sd((n, D_MODEL), BF16), sd((n, LANES), F32)],
        compiler_params=_cparams(("arbitrary",)),
        name="w_out",
    )(rc, ya, yb, yc, x, w_bf, gt, sc, sh, g, b, rw_pad)


def _ffn_kernel(x_ref, wg_ref, wu_ref, wd_ref, gate_ref, o_ref, *, n_j):
    j = pl.program_id(2)
    x = x_ref[0, 0]
    gg = _dot(x, wg_ref[0].astype(BF16))
    uu = _dot(x, wu_ref[0].astype(BF16))
    act = (_silu(gg) * uu).astype(BF16)
    part = _dot(act, wd_ref[0].astype(BF16))

    @pl.when(j == 0)
    def _():
        o_ref[0, 0] = part

    @pl.when(j > 0)
    def _():
        o_ref[0, 0] += part

    @pl.when(j == n_j - 1)
    def _():
        o_ref[0, 0] = o_ref[0, 0] * gate_ref[0, 0]


def _ffn_call(xs, gates, w_gate, w_up, w_down):
    e, two, rh, d = xs.shape
    tf = 256
    n_j = D_EXPERT // tf
    return pl.pallas_call(
        functools.partial(_ffn_kernel, n_j=n_j),
        grid=(e, two, n_j),
        in_specs=[pl.BlockSpec((1, 1, rh, d), lambda e, h, j: (e, h, 0, 0)),
                  pl.BlockSpec((1, d, tf), lambda e, h, j: (e, 0, j)),
                  pl.BlockSpec((1, d, tf), lambda e, h, j: (e, 0, j)),
                  pl.BlockSpec((1, tf, d), lambda e, h, j: (e, j, 0)),
                  pl.BlockSpec((1, 1, rh, 1), lambda e, h, j: (e, h, 0, 0))],
        out_specs=pl.BlockSpec((1, 1, rh, d), lambda e, h, j: (e, h, 0, 0)),
        out_shape=jax.ShapeDtypeStruct((e, two, rh, d), F32),
        compiler_params=_cparams(("arbitrary", "arbitrary", "arbitrary")),
        name="moe_ffn",
    )(xs, w_gate, w_up, w_down, gates)


def _final_ln_kernel(rc_ref, x_ref, m_ref, gt_ref, g_ref, b_ref, o_ref):
    o_ref[...] = _layer_norm(ALPHA * x_ref[...] + gt_ref[0] * m_ref[...], g_ref[...], b_ref[...])


def _final_ln_call(cfg, x1, m, gt, g, b):
    n = cfg.n_rows
    tm = cfg.seg
    rc = _row_cond(cfg, tm)
    row = pl.BlockSpec((tm, D_MODEL), lambda i, rc: (i, 0))
    gs = pltpu.PrefetchScalarGridSpec(
        num_scalar_prefetch=1,
        grid=(n // tm,),
        in_specs=[row, row, pl.BlockSpec((1, 1, D_MODEL), lambda i, rc: (rc[i], 0, 0)),
                  pl.BlockSpec((1, D_MODEL), lambda i, rc: (0, 0)),
                  pl.BlockSpec((1, D_MODEL), lambda i, rc: (0, 0))],
        out_specs=row,
    )
    return pl.pallas_call(
        _final_ln_kernel, grid_spec=gs,
        out_shape=jax.ShapeDtypeStruct((n, D_MODEL), F32),
        compiler_params=_cparams(("arbitrary",)),
        name="final_ln",
    )(rc, x1, m, gt, g, b)


def _block_diag_ones(n, blk):
    idx = np.arange(n) // blk
    return jnp.asarray((idx[:, None] == idx[None, :]).astype(np.float32))


def _rwkv_params(l, mu, w0, w_up, a0, a_up, g_up, k_k, k_a, r_k, gn_g, gn_b):
    z = jnp.zeros((R_W, D_A), F32)
    wup = jnp.stack([jnp.concatenate([w_up[l, 0], z], axis=0), jnp.concatenate([z, w_up[l, 1]], axis=0)])
    aup = jnp.stack([jnp.concatenate([a_up[l, 0], z], axis=0), jnp.concatenate([z, a_up[l, 1]], axis=0)])
    return dict(mu=mu[l][None], w0=w0[l], wup=wup, a0=a0[l], aup=aup, gup=g_up[l], k_k=k_k[l][None],
                k_a=k_a[l][None], r_k=r_k[l].reshape(1, D_A), gn_g=gn_g[l][None], gn_b=gn_b[l][None],
                bd=_block_diag_ones(D_A, HEAD_A))


def _pair_block_diag(w):
    w = w.reshape(D_B // LANES, 2, LRU_BLOCK, LRU_BLOCK)
    z = jnp.zeros_like(w[:, 0])
    top = jnp.concatenate([w[:, 0], z], axis=2)
    bot = jnp.concatenate([z, w[:, 1]], axis=2)
    return jnp.concatenate([top, bot], axis=1)


def _lru_params(l, conv_w, conv_b, w_a, b_a, w_x, b_x, lam):
    wg = jnp.concatenate([_pair_block_diag(w_a[l, 0]), _pair_block_diag(w_x[l, 0]),
                          _pair_block_diag(w_a[l, 1]), _pair_block_diag(w_x[l, 1])], axis=2)
    ncg = D_B // LANES
    bg = jnp.concatenate([b_a[l, 0].reshape(ncg, 1, LANES), b_x[l, 0].reshape(ncg, 1, LANES),
                          b_a[l, 1].reshape(ncg, 1, LANES), b_x[l, 1].reshape(ncg, 1, LANES)], axis=2)
    return dict(conv_w=conv_w[l], conv_b=conv_b[l][None], wg=wg, bg=bg, lam=lam[l])


def _ssd_params(l, conv_w, conv_b, dt_bias, a_log, d_skip, norm_g):
    n_bc = G_C * N_C
    pad = lambda v: jnp.pad(v, ((0, 0), (0, LANES - H_C)))
    hidx = np.arange(D_C) // P_C
    expand = jnp.asarray((np.arange(LANES)[:, None] == hidx[None, :]).astype(np.float32))
    cw, cb = conv_w[l], conv_b[l]
    return dict(cwx=cw[:, :D_C], cbx=cb[None, :D_C], cwb=cw[:, D_C:D_C + n_bc], cbb=cb[None, D_C:D_C + n_bc],
                cwc=cw[:, D_C + n_bc:], cbc=cb[None, D_C + n_bc:], dt_bias=pad(dt_bias[l]),
                a_log=pad(a_log[l]).reshape(2, 1, LANES), expand=expand,
                d_skip=jnp.repeat(d_skip[l], P_C)[None], norm_g=norm_g[l][None])


def _wkv_state_in(cfg, s_lat):
    n_hp = H_A // 2
    s = jnp.moveaxis(s_lat, 1, 0).reshape(2, cfg.n_lat, n_hp, 2, HEAD_A, HEAD_A)
    z = jnp.zeros_like(s[:, :, :, 0])
    top = jnp.concatenate([s[:, :, :, 0], z], axis=-1)
    bot = jnp.concatenate([z, s[:, :, :, 1]], axis=-1)
    lat = jnp.concatenate([top, bot], axis=-2)
    return jnp.concatenate([jnp.zeros((2, cfg.n_ctx, n_hp, LANES, LANES), F32), lat], axis=1)


def _wkv_state_out(cfg, sfin):
    s = sfin[:, :cfg.n_ctx]
    a = s[..., :HEAD_A, :HEAD_A]
    b = s[..., HEAD_A:, HEAD_A:]
    out = jnp.stack([a, b], axis=3).reshape(2, cfg.n_ctx, H_A, HEAD_A, HEAD_A)
    return jnp.moveaxis(out, 0, 1)


def _ssd_state_in(cfg, s_lat):
    n_hp = H_C // 2
    s = jnp.moveaxis(s_lat, 1, 0).reshape(2, cfg.n_lat, n_hp, 2, P_C, N_C)
    lat = jnp.transpose(s, (0, 1, 2, 5, 3, 4)).reshape(2, cfg.n_lat, n_hp, N_C, 2 * P_C)
    return jnp.concatenate([jnp.zeros((2, cfg.n_ctx, n_hp, N_C, 2 * P_C), F32), lat], axis=1)


def _ssd_state_out(cfg, sfin):
    n_hp = H_C // 2
    s = sfin[:, :cfg.n_ctx].reshape(2, cfg.n_ctx, n_hp, N_C, 2, P_C)
    s = jnp.transpose(s, (0, 1, 2, 4, 5, 3)).reshape(2, cfg.n_ctx, H_C, P_C, N_C)
    return jnp.moveaxis(s, 0, 1)


def _lru_state_in(cfg, s_lat):
    nseg = cfg.segs_per_group
    lat = jnp.zeros((cfg.n_lat, 2, nseg, D_B), F32).at[:, :, 0, :].set(s_lat)
    return jnp.concatenate([jnp.zeros((cfg.n_ctx // nseg, 2, nseg, D_B), F32), lat], axis=0)


def _lru_state_out(cfg, hfin):
    nseg = cfg.segs_per_group
    h = hfin[:cfg.n_ctx // nseg]
    return jnp.transpose(h, (0, 2, 1, 3)).reshape(cfg.n_ctx, 2, D_B)


def _grid_pos_embed(n_tokens):
    rows = n_tokens // GRID_W
    quarter = D_MODEL // 4
    omega = 1.0 / (10000.0 ** (jnp.arange(quarter, dtype=F32) / quarter))
    r_idx = jnp.broadcast_to(jnp.arange(rows, dtype=F32)[:, None], (rows, GRID_W)).reshape(-1)
    c_idx = jnp.broadcast_to(jnp.arange(GRID_W, dtype=F32)[None, :], (rows, GRID_W)).reshape(-1)
    ang_r = r_idx[:, None] * omega[None]
    ang_c = c_idx[:, None] * omega[None]
    return jnp.concatenate([jnp.sin(ang_r), jnp.cos(ang_r), jnp.sin(ang_c), jnp.cos(ang_c)], axis=-1)


def _route(cfg, logits):
    aff = jax.nn.softmax(logits[:, :N_EXPERTS], axis=-1)
    n_c = cfg.n_ctx * cfg.t_ctx
    rows, gates = [], []
    for base, nb, t in ((0, cfg.n_ctx, cfg.t_ctx), (n_c, cfg.n_lat, cfg.t_lat)):
        cap = CAP_FACTOR * t // N_EXPERTS
        a = aff[base:base + nb * t].reshape(nb, t, N_EXPERTS)
        gt, idx = lax.top_k(jnp.swapaxes(a, 1, 2), cap)
        idx = idx + (base + jnp.arange(nb, dtype=idx.dtype) * t)[:, None, None]
        rows.append(jnp.swapaxes(idx, 0, 1).reshape(N_EXPERTS, nb * cap))
        gates.append(jnp.swapaxes(gt, 0, 1).reshape(N_EXPERTS, nb * cap))
    return jnp.stack(rows, axis=1), jnp.stack(gates, axis=1)


def _layer(cfg, meta, l, x, mod, s_wkv, s_lru, s_ssd, w):
    sh1, sc1, gt1, sh2, sc2, gt2 = [mod[l][:, k * D_MODEL:(k + 1) * D_MODEL].reshape(SUBLANES, 1, D_MODEL)
                                    for k in range(6)]
    p = _win_call(cfg, x, sc1, sh1, _permute_w_in(w["w_in"][l]))

    ra = _rwkv_params(l, w["rwkv_mu"], w["rwkv_w0"], w["rwkv_w_up"], w["rwkv_a0"], w["rwkv_a_up"],
                      w["rwkv_g_up"], w["rwkv_k_k"], w["rwkv_k_a"], w["rwkv_r_k"], w["rwkv_gn_g"], w["rwkv_gn_b"])
    r, v, kk, g, wl, kd, a, bonus = _rwkv_prep_call(cfg, p, meta, ra)
    y2, sfin_a = _wkv_scan_call(cfg, meta, r, v, kk, wl, kd, a, s_wkv)
    y_a = _rwkv_post_call(cfg, y2, v, g, bonus, ra)

    rb = _lru_params(l, w["lru_conv_w"], w["lru_conv_b"], w["lru_w_a"], w["lru_b_a"], w["lru_w_x"],
                     w["lru_b_x"], w["lru_lambda"])
    y_b, hfin = _lru_call(cfg, p, rb, s_lru)

    rcp = _ssd_params(l, w["ssd_conv_w"], w["ssd_conv_b"], w["ssd_dt_bias"], w["ssd_a_log"], w["ssd_d"],
                      w["ssd_norm_g"])
    xc, bc, cc, dt2 = _ssd_prep_call(cfg, p, meta, rcp)
    ys2, sfin_c = _ssd_scan_call(cfg, meta, xc, bc, cc, dt2, rcp, s_ssd)
    y_c = _ssd_post_call(cfg, ys2, xc, p, rcp)

    rw_pad = jnp.pad(w["router_w"][l], ((0, 0), (0, LANES - N_EXPERTS)))
    x1, h2, logits = _wout_call(cfg, y_a, y_b, y_c, x, w["w_out"][l].astype(BF16), gt1, sc2, sh2,
                                w["ln1_g"][l][None], w["ln1_b"][l][None], rw_pad)

    rows, gates = _route(cfg, logits)
    xs = h2[rows]
    ye = _ffn_call(xs, gates[..., None], w["exp_w_gate"][l], w["exp_w_up"][l], w["exp_w_down"][l])
    m = jnp.zeros((cfg.n_rows, D_MODEL), F32).at[rows.reshape(-1)].add(ye.reshape(-1, D_MODEL))
    x2 = _final_ln_call(cfg, x1, m, gt2, w["ln2_g"][l][None], w["ln2_b"][l][None])
    return x2, sfin_a, hfin, sfin_c


def kernel(x_prompt, x_sample, c, c_ctx, state_rwkv, state_rglru, state_ssd, w_mod, b_mod, w_in, w_out, rwkv_mu, rwkv_w0, rwkv_w_up, rwkv_a0, rwkv_a_up, rwkv_g_up, rwkv_k_k, rwkv_k_a, rwkv_r_k, rwkv_gn_g, rwkv_gn_b, lru_conv_w, lru_conv_b, lru_w_a, lru_b_a, lru_w_x, lru_b_x, lru_lambda, ssd_conv_w, ssd_conv_b, ssd_dt_bias, ssd_a_log, ssd_d, ssd_norm_g, ln1_g, ln1_b, ln2_g, ln2_b, router_w, exp_w_gate, exp_w_up, exp_w_down):
    cfg = Cfg(x_prompt.shape[0], x_prompt.shape[1], x_sample.shape[0], x_sample.shape[1])
    _check_cfg(cfg)
    assert cfg.n_lat + 1 <= SUBLANES
    w = dict(w_in=w_in, w_out=w_out, rwkv_mu=rwkv_mu, rwkv_w0=rwkv_w0, rwkv_w_up=rwkv_w_up, rwkv_a0=rwkv_a0,
             rwkv_a_up=rwkv_a_up, rwkv_g_up=rwkv_g_up, rwkv_k_k=rwkv_k_k, rwkv_k_a=rwkv_k_a, rwkv_r_k=rwkv_r_k,
             rwkv_gn_g=rwkv_gn_g, rwkv_gn_b=rwkv_gn_b, lru_conv_w=lru_conv_w, lru_conv_b=lru_conv_b,
             lru_w_a=lru_w_a, lru_b_a=lru_b_a, lru_w_x=lru_w_x, lru_b_x=lru_b_x, lru_lambda=lru_lambda,
             ssd_conv_w=ssd_conv_w, ssd_conv_b=ssd_conv_b, ssd_dt_bias=ssd_dt_bias, ssd_a_log=ssd_a_log,
             ssd_d=ssd_d, ssd_norm_g=ssd_norm_g, ln1_g=ln1_g, ln1_b=ln1_b, ln2_g=ln2_g, ln2_b=ln2_b,
             router_w=router_w, exp_w_gate=exp_w_gate, exp_w_up=exp_w_up, exp_w_down=exp_w_down)
    meta = _tile_meta(cfg)
    n_c = cfg.n_ctx * cfg.t_ctx

    cond8 = jnp.concatenate([c_ctx[None], c, jnp.zeros((SUBLANES - 1 - cfg.n_lat, D_MODEL), F32)], axis=0)
    mod = _mod_call(cond8, w_mod, b_mod)

    xs = x_sample + _grid_pos_embed(cfg.t_lat)[None]
    x = jnp.concatenate([x_prompt.reshape(n_c, D_MODEL), xs.reshape(n_c, D_MODEL)], axis=0)

    new_a, new_b, new_c = [], [], []
    for l in range(DEPTH):
        x, sfin_a, hfin, sfin_c = _layer(cfg, meta, l, x, mod,
                                         _wkv_state_in(cfg, state_rwkv[:, l]),
                                         _lru_state_in(cfg, state_rglru[:, l]),
                                         _ssd_state_in(cfg, state_ssd[:, l]), w)
        new_a.append(_wkv_state_out(cfg, sfin_a))
        new_b.append(_lru_state_out(cfg, hfin))
        new_c.append(_ssd_state_out(cfg, sfin_c))

    y_prompt = x[:n_c].reshape(x_prompt.shape)
    y_sample = x[n_c:].reshape(x_sample.shape)
    return (y_prompt, y_sample, jnp.stack(new_a, axis=1), jnp.stack(new_b, axis=1), jnp.stack(new_c, axis=1))
```
